```python
import math
import jax, jax.numpy as jnp
from jax import lax
import numpy as np

D_MODEL = 1024
BATCH = 8
SEQ = 4096
DEPTH = 2
DEC_BATCH = 32
DEC_SEQ = 1
PAST_LEN = 16384
PAGE_SIZE = 128

MIX_W = D_MODEL // 4
HEAD_DIM = 64
H_MOBA = MIX_W // HEAD_DIM
H_DIFF = MIX_W // HEAD_DIM
DIFF_QK = HEAD_DIM // 2
MOBA_BLOCK = 256
MOBA_TOPK = 3
SC_W = 3
CF_W = 31
N_BRANCH = 4
N_SPLIT = 11
IN_W = N_SPLIT * MIX_W + N_BRANCH * D_MODEL
D_FF = ((8 * D_MODEL // 3 + 127) // 128) * 128
N_BUCKETS = 32
MAX_DIST = 128
Q_BLOCK = 128
N_NORMS = 6
EPS = 1e-6

kernel_name = 'hybrid_moba_diffattn_conv_decoder_step'


def rms_norm(x, g):
    xf = x.astype(jnp.float32)
    y = xf * lax.rsqrt(jnp.mean(xf * xf, axis=-1, keepdims=True) + EPS)
    return (y * g.astype(jnp.float32)).astype(x.dtype)


def layer_norm(x, g, b):
    xf = x.astype(jnp.float32)
    mu = jnp.mean(xf, axis=-1, keepdims=True)
    xc = xf - mu
    y = xc * lax.rsqrt(jnp.mean(xc * xc, axis=-1, keepdims=True) + EPS)
    return (y * g.astype(jnp.float32) + b.astype(jnp.float32)).astype(x.dtype)


def swiglu(x, w_up, w_down):
    gate, up = jnp.split(x @ w_up, 2, axis=-1)
    return (jax.nn.silu(gate) * up) @ w_down


def t5_bucket(dist):
    n = jnp.maximum(dist, 0)
    exact = N_BUCKETS // 2
    nf = jnp.maximum(n, 1).astype(jnp.float32)
    large = exact + (jnp.log(nf / exact) / math.log(MAX_DIST / exact) * (N_BUCKETS - exact)).astype(jnp.int32)
    large = jnp.minimum(large, N_BUCKETS - 1)
    return jnp.where(n < exact, n, large)


def split_in(z):
    parts = jnp.split(z[..., :N_SPLIT * MIX_W], N_SPLIT, axis=-1)
    return parts, z[..., N_SPLIT * MIX_W:]


def causal_dwconv(u, buf, w):
    ext = jnp.concatenate([buf.astype(u.dtype), u], axis=1)
    y = lax.conv_general_dilated(ext, w[:, None, :].astype(u.dtype), window_strides=(1,), padding='VALID',
                                 dimension_numbers=('NWC', 'WIO', 'NWC'), feature_group_count=u.shape[-1])
    return y, ext[:, ext.shape[1] - (w.shape[0] - 1):]


def short_conv_branch(h, gb, gc, buf, w):
    y, new_buf = causal_dwconv(gc * h, buf, w)
    return gb * y, new_buf


def conformer_branch(a, b, buf, w, ln_g, ln_b):
    y, new_buf = causal_dwconv(a * jax.nn.sigmoid(b), buf, w)
    return jax.nn.silu(layer_norm(y, ln_g, ln_b)), new_buf


def to_blocks(x):
    t = x.shape[1]
    nb = -(-t // MOBA_BLOCK)
    x = jnp.pad(x, ((0, 0), (0, nb * MOBA_BLOCK - t), (0, 0), (0, 0)))
    return x.reshape((x.shape[0], nb, MOBA_BLOCK) + x.shape[2:])


def moba_attend(q, q_pos, k_blk, v_blk, k_mean, tab):
    nq, nh = q.shape[0], q.shape[1]
    nb = k_blk.shape[0]
    own = q_pos // MOBA_BLOCK
    score = jnp.einsum('qhd,nhd->qhn', q.astype(jnp.float32), k_mean)
    is_past = jnp.arange(nb)[None, None, :] < own[:, None, None]
    score = jnp.where(is_past, score, -jnp.inf)
    _, top = lax.top_k(score, min(MOBA_TOPK, nb))
    sel = jnp.concatenate([top, jnp.broadcast_to(own[:, None, None], (nq, nh, 1))], axis=-1)
    ok = jnp.concatenate([top < own[:, None, None], jnp.ones((nq, nh, 1), dtype=bool)], axis=-1)
    hidx = jnp.arange(nh)[None, :, None]
    kg = jnp.moveaxis(k_blk, 2, 0)[hidx, sel]
    vg = jnp.moveaxis(v_blk, 2, 0)[hidx, sel]
    k_pos = sel[..., None] * MOBA_BLOCK + jnp.arange(MOBA_BLOCK)
    dist = q_pos[:, None, None, None] - k_pos
    bias = tab.T[hidx[..., None], t5_bucket(dist)].astype(jnp.float32)
    logits = jnp.einsum('qhd,qhnkd->qhnk', q, kg).astype(jnp.float32) * HEAD_DIM ** -0.5 + bias
    logits = jnp.where(ok[..., None] & (dist >= 0), logits, -jnp.inf)
    p = jax.nn.softmax(logits.reshape(nq, nh, -1), axis=-1).reshape(logits.shape)
    return jnp.einsum('qhnk,qhnkd->qhd', p.astype(vg.dtype), vg)


def moba_prompt(q, k, v, tab):
    nbat, s = q.shape[0], q.shape[1]
    kb, vb = to_blocks(k), to_blocks(v)
    km = jnp.mean(kb.astype(jnp.float32), axis=2)
    n_qb = s // Q_BLOCK

    def one(i):
        b, c = i // n_qb, i % n_qb
        qc = lax.dynamic_slice_in_dim(q[b], c * Q_BLOCK, Q_BLOCK, axis=0)
        pos = c * Q_BLOCK + jnp.arange(Q_BLOCK, dtype=jnp.int32)
        return moba_attend(qc, pos, kb[b], vb[b], km[b], tab)

    out = lax.map(one, jnp.arange(nbat * n_qb, dtype=jnp.int32))
    return out.reshape(nbat, s, -1)


def moba_sample(q, k_all, v_all, q_pos, tab):
    kb, vb = to_blocks(k_all), to_blocks(v_all)
    km = jnp.mean(kb.astype(jnp.float32), axis=2)
    out = jax.vmap(lambda qq, kk, vv, mm: moba_attend(qq, q_pos, kk, vv, mm, tab))(q, kb, vb, km)
    return out.reshape(q.shape[0], q.shape[1], -1)


def diff_attend(q, k, v, q_pos, k_pos, lam, tab):
    dist = q_pos[:, None] - k_pos[None, :]
    bias = jnp.moveaxis(tab[t5_bucket(dist)], -1, 0).astype(jnp.float32)
    logits = jnp.einsum('bqhcd,bkhcd->bchqk', q, k).astype(jnp.float32) * DIFF_QK ** -0.5 + bias[None, None]
    logits = jnp.where((dist >= 0)[None, None, None], logits, -jnp.inf)
    p = jax.nn.softmax(logits, axis=-1)
    a = p[:, 0] - lam * p[:, 1]
    return jnp.einsum('bhqk,bkhd->bqhd', a.astype(v.dtype), v)


def diff_prompt(q, k, v, pos, lam, tab):
    nbat, s = q.shape[0], q.shape[1]

    def one(c):
        qc = lax.dynamic_slice_in_dim(q, c * Q_BLOCK, Q_BLOCK, axis=1)
        qpos = c * Q_BLOCK + jnp.arange(Q_BLOCK, dtype=jnp.int32)
        return diff_attend(qc, k, v, qpos, pos, lam, tab)

    out = lax.map(one, jnp.arange(s // Q_BLOCK, dtype=jnp.int32))
    return jnp.moveaxis(out, 0, 1).reshape(nbat, s, out.shape[3], out.shape[4])


def diff_out(o, g, lam_init):
    o = rms_norm(o, g) * (1.0 - lam_init)
    return o.reshape(o.shape[0], o.shape[1], -1)


def merge(branches, gates, w_br, w_o):
    acc = None
    for n, o in enumerate(branches):
        term = jax.nn.sigmoid(gates[..., n * D_MODEL:(n + 1) * D_MODEL]) * (o @ w_br[n])
        acc = term if acc is None else acc + term
    return acc @ w_o


def setup_inputs(seed: int = 0) -> dict:
    key = jax.random.key(seed)
    ks = jax.random.split(key, 22)
    f32 = jnp.float32
    n_pages = PAST_LEN // PAGE_SIZE
    n_used = DEC_BATCH * n_pages
    n_pool = n_used + n_used // 4

    def nrm(k, shape, scale=1.0):
        return jax.random.normal(k, shape, f32) * scale

    def gain(k, shape):
        return 1.0 + 0.02 * jax.random.normal(k, shape, f32)

    page_table = jax.random.permutation(ks[6], n_pool)[:n_used].reshape(DEC_BATCH, n_pages).astype(jnp.int32)
    return {
        'x_prompt': nrm(ks[0], (BATCH, SEQ, D_MODEL)),
        'x_sample': nrm(ks[1], (DEC_BATCH, DEC_SEQ, D_MODEL)),
        'cache_kv_moba': nrm(ks[2], (DEPTH, n_pool, PAGE_SIZE, 2, H_MOBA, HEAD_DIM)),
        'cache_kv_diff': nrm(ks[3], (DEPTH, n_pool, PAGE_SIZE, 2, H_DIFF, HEAD_DIM)),
        'state_conv_short': nrm(ks[4], (DEPTH, DEC_BATCH, SC_W - 1, MIX_W)),
        'state_conv_conformer': nrm(ks[5], (DEPTH, DEC_BATCH, CF_W - 1, MIX_W)),
        'page_table': page_table,
        'rel_bias_table': nrm(ks[7], (N_BUCKETS, H_MOBA + H_DIFF), 0.5),
        'norm_g': gain(ks[8], (DEPTH, N_NORMS, D_MODEL)),
        'w_ffn1_up': nrm(ks[9], (DEPTH, D_MODEL, 2 * D_FF), D_MODEL ** -0.5),
        'w_ffn1_down': nrm(ks[10], (DEPTH, D_FF, D_MODEL), D_FF ** -0.5),
        'w_ffn2_up': nrm(ks[11], (DEPTH, D_MODEL, 2 * D_FF), D_MODEL ** -0.5),
        'w_ffn2_down': nrm(ks[12], (DEPTH, D_FF, D_MODEL), D_FF ** -0.5),
        'w_in': nrm(ks[13], (DEPTH, D_MODEL, IN_W), D_MODEL ** -0.5),
        'diff_lambda': nrm(ks[14], (DEPTH, 4, DIFF_QK), 0.1),
        'diff_subln_g': gain(ks[15], (DEPTH, HEAD_DIM)),
        'w_conv_short': nrm(ks[16], (DEPTH, SC_W, MIX_W), SC_W ** -0.5),
        'w_conv_cf': nrm(ks[17], (DEPTH, CF_W, MIX_W), CF_W ** -0.5),
        'cf_ln_g': gain(ks[18], (DEPTH, MIX_W)),
        'cf_ln_b': nrm(ks[19], (DEPTH, MIX_W), 0.02),
        'w_branch': nrm(ks[20], (DEPTH, N_BRANCH, MIX_W, D_MODEL), MIX_W ** -0.5),
        'w_out': nrm(ks[21], (DEPTH, D_MODEL, D_MODEL), D_MODEL ** -0.5),
    }


def reference(x_prompt, x_sample, cache_kv_moba, cache_kv_diff, state_conv_short, state_conv_conformer,
              page_table, rel_bias_table, norm_g, w_ffn1_up, w_ffn1_down, w_ffn2_up, w_ffn2_down, w_in,
              diff_lambda, diff_subln_g, w_conv_short, w_conv_cf, cf_ln_g, cf_ln_b, w_branch, w_out):
    nbat, s = x_prompt.shape[0], x_prompt.shape[1]
    dbat, q_len = x_sample.shape[0], x_sample.shape[1]
    t_all = PAST_LEN + q_len
    p_pos = jnp.arange(s, dtype=jnp.int32)
    s_pos = PAST_LEN + jnp.arange(q_len, dtype=jnp.int32)
    k_pos_s = jnp.arange(t_all, dtype=jnp.int32)
    tab_a = rel_bias_table[:, :H_MOBA]
    tab_d = rel_bias_table[:, H_MOBA:]
    yp, ys = x_prompt, x_sample
    kvm_p, kvd_p, cs_p, cc_p = [], [], [], []
    kvm_s, kvd_s, cs_s, cc_s = [], [], [], []
    for l in range(DEPTH):
        g = norm_g[l]
        lam_init = 0.8 - 0.6 * math.exp(-0.3 * l)
        lp = diff_lambda[l].astype(jnp.float32)
        lam = jnp.exp(jnp.sum(lp[0] * lp[1])) - jnp.exp(jnp.sum(lp[2] * lp[3])) + lam_init

        yp = yp + 0.5 * rms_norm(swiglu(rms_norm(yp, g[0]), w_ffn1_up[l], w_ffn1_down[l]), g[1])
        ys = ys + 0.5 * rms_norm(swiglu(rms_norm(ys, g[0]), w_ffn1_up[l], w_ffn1_down[l]), g[1])

        parts, gates = split_in(rms_norm(yp, g[2]) @ w_in[l])
        qa, ka, va, qd, kd, vd, hb, bb, cb, ga, gg = parts
        ka4 = ka.reshape(nbat, s, H_MOBA, HEAD_DIM)
        va4 = va.reshape(nbat, s, H_MOBA, HEAD_DIM)
        oa = moba_prompt(qa.reshape(nbat, s, H_MOBA, HEAD_DIM), ka4, va4, tab_a)
        kd4 = kd.reshape(nbat, s, H_DIFF, HEAD_DIM)
        vd4 = vd.reshape(nbat, s, H_DIFF, HEAD_DIM)
        od = diff_prompt(qd.reshape(nbat, s, H_DIFF, 2, DIFF_QK), kd.reshape(nbat, s, H_DIFF, 2, DIFF_QK),
                         vd4, p_pos, lam, tab_d)
        od = diff_out(od, diff_subln_g[l], lam_init)
        ob, buf_b = short_conv_branch(hb, bb, cb, jnp.zeros((nbat, SC_W - 1, MIX_W), yp.dtype), w_conv_short[l])
        oc, buf_c = conformer_branch(ga, gg, jnp.zeros((nbat, CF_W - 1, MIX_W), yp.dtype), w_conv_cf[l],
                                     cf_ln_g[l], cf_ln_b[l])
        yp = yp + rms_norm(merge((oa, ob, oc, od), gates, w_branch[l], w_out[l]), g[3])
        kvm_p.append(jnp.stack([ka4, va4], axis=2))
        kvd_p.append(jnp.stack([kd4, vd4], axis=2))
        cs_p.append(buf_b)
        cc_p.append(buf_c)

        parts, gates = split_in(rms_norm(ys, g[2]) @ w_in[l])
        qa, ka, va, qd, kd, vd, hb, bb, cb, ga, gg = parts
        new_m = jnp.stack([ka.reshape(dbat, q_len, H_MOBA, HEAD_DIM), va.reshape(dbat, q_len, H_MOBA, HEAD_DIM)], axis=2)
        past_m = cache_kv_moba[l][page_table].reshape(dbat, PAST_LEN, 2, H_MOBA, HEAD_DIM).astype(new_m.dtype)
        all_m = jnp.concatenate([past_m, new_m], axis=1)
        oa = moba_sample(qa.reshape(dbat, q_len, H_MOBA, HEAD_DIM), all_m[:, :, 0], all_m[:, :, 1], s_pos, tab_a)
        new_d = jnp.stack([kd.reshape(dbat, q_len, H_DIFF, HEAD_DIM), vd.reshape(dbat, q_len, H_DIFF, HEAD_DIM)], axis=2)
        past_d = cache_kv_diff[l][page_table].reshape(dbat, PAST_LEN, 2, H_DIFF, HEAD_DIM).astype(new_d.dtype)
        all_d = jnp.concatenate([past_d, new_d], axis=1)
        od = diff_attend(qd.reshape(dbat, q_len, H_DIFF, 2, DIFF_QK),
                         all_d[:, :, 0].reshape(dbat, t_all, H_DIFF, 2, DIFF_QK), all_d[:, :, 1],
                         s_pos, k_pos_s, lam, tab_d)
        od = diff_out(od, diff_subln_g[l], lam_init)
        ob, buf_b = short_conv_branch(hb, bb, cb, state_conv_short[l], w_conv_short[l])
        oc, buf_c = conformer_branch(ga, gg, state_conv_conformer[l], w_conv_cf[l], cf_ln_g[l], cf_ln_b[l])
        ys = ys + rms_norm(merge((oa, ob, oc, od), gates, w_branch[l], w_out[l]), g[3])
        kvm_s.append(new_m)
        kvd_s.append(new_d)
        cs_s.append(buf_b)
        cc_s.append(buf_c)

        yp = yp + 0.5 * rms_norm(swiglu(rms_norm(yp, g[4]), w_ffn2_up[l], w_ffn2_down[l]), g[5])
        ys = ys + 0.5 * rms_norm(swiglu(rms_norm(ys, g[4]), w_ffn2_up[l], w_ffn2_down[l]), g[5])

    return (yp, ys, jnp.stack(kvm_p), jnp.stack(kvd_p), jnp.stack(cs_p), jnp.stack(cc_p),
            jnp.stack(kvm_s), jnp.stack(kvd_s), jnp.stack(cs_s), jnp.stack(cc_s))
```

```python
import functools
import math

import jax
import jax.numpy as jnp
import numpy as np
from jax import lax
from jax.experimental import pallas as pl
from jax.experimental.pallas import tpu as pltpu

F32 = jnp.float32
BF16 = jnp.bfloat16
EPS = 1e-6
MIX_W = 256
HEAD_DIM = 64
N_HEADS = MIX_W // HEAD_DIM
DIFF_QK = HEAD_DIM // 2
N_SEG = MIX_W // DIFF_QK
MOBA_BLOCK = 256
MOBA_TOPK = 3
N_BUCKETS = 32
MAX_DIST = 128
N_SPLIT = 11
N_BRANCH = 4
LANES = 128
ATT_TILE = MOBA_BLOCK
CONV_HALO = 32
CONV_ROWS = 64
NEG = -1e30
HIGHEST = lax.Precision.HIGHEST
NT_DIMS = (((1,), (1,)), ((), ()))


def _t5_thresholds():
    exact = N_BUCKETS // 2
    n = np.arange(0, 4 * MAX_DIST)
    nf = np.maximum(n, 1).astype(np.float32)
    large = exact + (np.log(nf / exact) / math.log(MAX_DIST / exact) * (N_BUCKETS - exact)).astype(np.int32)
    bucket = np.where(n < exact, n, np.minimum(large, N_BUCKETS - 1))
    assert np.all(np.diff(bucket) >= 0) and bucket[-1] == N_BUCKETS - 1
    return [int(np.argmax(bucket >= b)) for b in range(1, N_BUCKETS)]


T5_THR = _t5_thresholds()


def _t5_bias(dist, tab_at):
    val = jnp.where(dist >= T5_THR[0], tab_at(1), tab_at(0))
    for b in range(2, N_BUCKETS):
        val = jnp.where(dist >= T5_THR[b - 1], tab_at(b), val)
    return val


def _rms(x, g):
    return x * lax.rsqrt(jnp.mean(x * x, axis=-1, keepdims=True) + EPS) * g


def _widen(x, n):
    reps = n // LANES
    return x if reps == 1 else jnp.concatenate([x] * reps, axis=1)


def _per_head(vals, hid):
    out = vals[0]
    for h in range(1, len(vals)):
        out = jnp.where(hid == h, vals[h], out)
    return out


def _const_spec(shape):
    return pl.BlockSpec(shape, lambda *_: (0,) * len(shape), pipeline_mode=pl.Buffered(1))


def _smem_spec():
    return pl.BlockSpec(memory_space=pltpu.SMEM)


def _ffn_body(x_ref, g_ref, wu_ref, wd_ref, o_ref, *, d_ff, chunk):
    x = x_ref[...]
    h = _rms(x, g_ref[0:1, :]).astype(BF16)
    acc = None
    for c in range(d_ff // chunk):
        gate = jnp.dot(h, wu_ref[:, c * chunk:(c + 1) * chunk], preferred_element_type=F32)
        up = jnp.dot(h, wu_ref[:, d_ff + c * chunk:d_ff + (c + 1) * chunk], preferred_element_type=F32)
        a = (gate * jax.nn.sigmoid(gate) * up).astype(BF16)
        term = jnp.dot(a, wd_ref[c * chunk:(c + 1) * chunk, :], preferred_element_type=F32)
        acc = term if acc is None else acc + term
    o_ref[...] = x + 0.5 * _rms(acc, g_ref[1:2, :])


def _ffn(x, g2, w_up, w_down, *, tm):
    n, d = x.shape
    d_ff = w_down.shape[0]
    chunk = 256 if d_ff % 256 == 0 else 128
    return pl.pallas_call(
        functools.partial(_ffn_body, d_ff=d_ff, chunk=chunk),
        grid=(n // tm,),
        in_specs=[pl.BlockSpec((tm, d), lambda i: (i, 0)),
                  _const_spec((2, d)), _const_spec((d, 2 * d_ff)), _const_spec((d_ff, d))],
        out_specs=pl.BlockSpec((tm, d), lambda i: (i, 0)),
        out_shape=jax.ShapeDtypeStruct((n, d), F32),
        compiler_params=pltpu.CompilerParams(dimension_semantics=("arbitrary",)),
        name="ffn",
    )(x, g2, w_up, w_down)


def _merge_body(x_ref, oa_ref, ob_ref, oc_ref, od_ref, g_ref, wg_ref, wbr_ref, wo_ref, o_ref):
    x = x_ref[...]
    d = x.shape[-1]
    h = _rms(x, g_ref[0:1, :]).astype(BF16)
    acc = None
    for n, br in enumerate((oa_ref, ob_ref, oc_ref, od_ref)):
        gate = jax.nn.sigmoid(jnp.dot(h, wg_ref[:, n * d:(n + 1) * d], preferred_element_type=F32))
        term = gate * jnp.dot(br[...], wbr_ref[n], preferred_element_type=F32)
        acc = term if acc is None else acc + term
    m = jnp.dot(acc.astype(BF16), wo_ref[...], preferred_element_type=F32)
    o_ref[...] = x + _rms(m, g_ref[1:2, :])


def _merge(x, oa, ob, oc, od, g2, w_gate, w_br, w_o, *, tm):
    n, d = x.shape
    row = lambda w: pl.BlockSpec((tm, w), lambda i: (i, 0))
    return pl.pallas_call(
        _merge_body,
        grid=(n // tm,),
        in_specs=[row(d), row(MIX_W), row(MIX_W), row(MIX_W), row(MIX_W),
                  _const_spec((2, d)), _const_spec((d, N_BRANCH * d)), _const_spec((N_BRANCH, MIX_W, d)),
                  _const_spec((d, d))],
        out_specs=row(d),
        out_shape=jax.ShapeDtypeStruct((n, d), F32),
        compiler_params=pltpu.CompilerParams(dimension_semantics=("arbitrary",)),
        name="merge",
    )(x, oa, ob, oc, od, g2, w_gate, w_br, w_o)


def _bias_body(tab_ref, o_ref):
    h = pl.program_id(0)
    i = lax.broadcasted_iota(jnp.int32, (ATT_TILE, ATT_TILE), 0)
    j = lax.broadcasted_iota(jnp.int32, (ATT_TILE, ATT_TILE), 1)
    for r in range(2):
        o_ref[0, r] = _t5_bias(r * ATT_TILE + i - j, lambda b: tab_ref[b, h])


def _bias_tiles(tab):
    nh = tab.shape[1]
    return pl.pallas_call(
        _bias_body,
        grid=(nh,),
        in_specs=[_smem_spec()],
        out_specs=pl.BlockSpec((1, 2, ATT_TILE, ATT_TILE), lambda h: (h, 0, 0, 0)),
        out_shape=jax.ShapeDtypeStruct((nh, 2, ATT_TILE, ATT_TILE), F32),
        name="t5_bias_tiles",
    )(tab)


def _conv_rows(e_ref, w_ref, width, r0, rows):
    acc = None
    for k in range(width):
        s = CONV_HALO - (width - 1) + k + r0
        term = e_ref[s:s + rows, :] * w_ref[k:k + 1, :]
        acc = term if acc is None else acc + term
    return acc


def _inproj_body(x_ref, g_ref, w_ref, wkv_ref, wcs_ref, wcc_ref, lng_ref, lnb_ref,
                 qa_ref, kvm_ref, kvmb_ref, km_ref, qd_ref, kvd_ref, kvdb_ref, ob_ref, oc_ref, cs_ref, cc_ref,
                 eb_ref, ec_ref, bb_ref, *, tm, sc_w, cf_w):
    t = pl.program_id(1)
    nbt = tm // MOBA_BLOCK
    h = _rms(x_ref[...], g_ref[...]).astype(BF16)

    def part(n):
        return jnp.dot(h, w_ref[:, n * MIX_W:(n + 1) * MIX_W], preferred_element_type=F32)

    kv_t = lax.dot_general(wkv_ref[...], h, NT_DIMS, preferred_element_type=F32)
    kvm_ref[...] = kv_t[0:2 * MIX_W]
    kvd_ref[...] = kv_t[2 * MIX_W:4 * MIX_W]
    lane = lax.broadcasted_iota(jnp.int32, km_ref.shape, 1)

    @pl.when(t == 0)
    def _():
        km_ref[...] = jnp.zeros(km_ref.shape, F32)

    for r in range(nbt):
        cols = slice(r * MOBA_BLOCK, (r + 1) * MOBA_BLOCK)
        kvmb_ref[r] = kv_t[0:2 * MIX_W, cols].astype(BF16)
        kvdb_ref[r] = kv_t[2 * MIX_W:4 * MIX_W, cols].astype(BF16)
        k_mean = jnp.sum(kv_t[0:MIX_W, cols], axis=1, keepdims=True) * (1.0 / MOBA_BLOCK)
        km_ref[...] = jnp.where(lane == t * nbt + r, k_mean, km_ref[...])

    qa_ref[...] = part(0)
    qd_ref[...] = (part(3) * DIFF_QK ** -0.5).astype(BF16)

    @pl.when(t == 0)
    def _():
        eb_ref[0:CONV_HALO, :] = jnp.zeros((CONV_HALO, MIX_W), F32)
        ec_ref[0:CONV_HALO, :] = jnp.zeros((CONV_HALO, MIX_W), F32)

    @pl.when(t > 0)
    def _():
        eb_ref[0:CONV_HALO, :] = eb_ref[tm:tm + CONV_HALO, :]
        ec_ref[0:CONV_HALO, :] = ec_ref[tm:tm + CONV_HALO, :]

    hb = part(6)
    bb_ref[...] = part(7)
    eb_ref[CONV_HALO:CONV_HALO + tm, :] = part(8) * hb
    ga = part(9)
    ec_ref[CONV_HALO:CONV_HALO + tm, :] = ga * jax.nn.sigmoid(part(10))

    for r0 in range(0, tm, CONV_ROWS):
        yb = _conv_rows(eb_ref, wcs_ref, sc_w, r0, CONV_ROWS)
        ob_ref[r0:r0 + CONV_ROWS, :] = (bb_ref[r0:r0 + CONV_ROWS, :] * yb).astype(BF16)
        yc = _conv_rows(ec_ref, wcc_ref, cf_w, r0, CONV_ROWS)
        mu = jnp.mean(yc, axis=-1, keepdims=True)
        xc = yc - mu
        yn = xc * lax.rsqrt(jnp.mean(xc * xc, axis=-1, keepdims=True) + EPS) * lng_ref[...] + lnb_ref[...]
        oc_ref[r0:r0 + CONV_ROWS, :] = (yn * jax.nn.sigmoid(yn)).astype(BF16)

    cs_ref[...] = eb_ref[CONV_HALO + tm - (sc_w - 1):CONV_HALO + tm, :]
    cc_ref[...] = ec_ref[CONV_HALO + tm - (cf_w - 1):CONV_HALO + tm, :]


def _inproj(x, g, w_mix, w_kv_t, w_cs, w_cc, ln_g, ln_b, *, tm):
    b, s, d = x.shape
    sc_w, cf_w = w_cs.shape[0], w_cc.shape[0]
    assert cf_w - 1 <= CONV_HALO and tm % MOBA_BLOCK == 0 and s % tm == 0
    nbt, nblk = tm // MOBA_BLOCK, s // MOBA_BLOCK
    tok = lambda w: pl.BlockSpec((None, tm, w), lambda bi, ti: (bi, ti, 0))
    tok_t = pl.BlockSpec((None, 2 * MIX_W, tm), lambda bi, ti: (bi, 0, ti))
    blk_t = pl.BlockSpec((None, nbt, 2 * MIX_W, MOBA_BLOCK), lambda bi, ti: (bi, ti, 0, 0))
    per_b = lambda r, c: pl.BlockSpec((None, r, c), lambda bi, ti: (bi, 0, 0))
    sds = jax.ShapeDtypeStruct
    return pl.pallas_call(
        functools.partial(_inproj_body, tm=tm, sc_w=sc_w, cf_w=cf_w),
        grid=(b, s // tm),
        in_specs=[tok(d), _const_spec((1, d)), _const_spec((d, N_SPLIT * MIX_W)), _const_spec((4 * MIX_W, d)),
                  _const_spec((sc_w, MIX_W)), _const_spec((cf_w, MIX_W)),
                  _const_spec((1, MIX_W)), _const_spec((1, MIX_W))],
        out_specs=[tok(MIX_W), tok_t, blk_t, per_b(MIX_W, nblk),
                   tok(MIX_W), tok_t, blk_t, tok(MIX_W), tok(MIX_W),
                   per_b(sc_w - 1, MIX_W), per_b(cf_w - 1, MIX_W)],
        out_shape=[sds((b, s, MIX_W), F32), sds((b, 2 * MIX_W, s), F32), sds((b, nblk, 2 * MIX_W, MOBA_BLOCK), BF16),
                   sds((b, MIX_W, nblk), F32),
                   sds((b, s, MIX_W), BF16), sds((b, 2 * MIX_W, s), F32), sds((b, nblk, 2 * MIX_W, MOBA_BLOCK), BF16),
                   sds((b, s, MIX_W), BF16), sds((b, s, MIX_W), BF16),
                   sds((b, sc_w - 1, MIX_W), F32), sds((b, cf_w - 1, MIX_W), F32)],
        scratch_shapes=[pltpu.VMEM((CONV_HALO + tm, MIX_W), F32), pltpu.VMEM((CONV_HALO + tm, MIX_W), F32),
                        pltpu.VMEM((tm, MIX_W), F32)],
        compiler_params=pltpu.CompilerParams(dimension_semantics=("arbitrary", "arbitrary")),
        name="inproj_prompt",
    )(x, g, w_mix, w_kv_t, w_cs, w_cc, ln_g, ln_b)


def _softmax_step(s, m_ref, l_ref, idx):
    m_prev = m_ref[idx]
    m_new = jnp.maximum(m_prev, jnp.max(s, axis=1, keepdims=True))
    p = jnp.exp(s - _widen(m_new, s.shape[1]))
    alpha = jnp.exp(m_prev - m_new)
    l_ref[idx] = alpha * l_ref[idx] + jnp.sum(p, axis=1, keepdims=True)
    m_ref[idx] = m_new
    return p, alpha


def _causal_tile():
    i = lax.broadcasted_iota(jnp.int32, (ATT_TILE, ATT_TILE), 0)
    j = lax.broadcasted_iota(jnp.int32, (ATT_TILE, ATT_TILE), 1)
    return i >= j


def _kv_tile(kv_ref, j):
    return kv_ref[j, 0:MIX_W, :], kv_ref[j, MIX_W:2 * MIX_W, :]


def _moba_body(tab_ref, q_ref, kv_ref, km_ref, bias_ref, o_ref, m_scr, l_scr, acc_scr, bits_scr, qb_scr, *, nblk):
    own = pl.program_id(1)
    q = q_ref[...]
    km = km_ref[...]
    hid = lax.broadcasted_iota(jnp.int32, (1, MIX_W), 1) // HEAD_DIM
    blk = lax.broadcasted_iota(jnp.int32, (ATT_TILE, nblk), 1)
    blk_f = blk.astype(F32)

    for h in range(N_HEADS):
        qh = jnp.where(hid == h, q, 0.0)
        sc = jnp.dot(qh, km, precision=HIGHEST, preferred_element_type=F32)
        sc = jnp.where(blk < own, sc, NEG)
        bits = jnp.zeros((ATT_TILE, 1), jnp.int32)
        for _ in range(MOBA_TOPK):
            mx = jnp.max(sc, axis=1, keepdims=True)
            first = jnp.min(jnp.where(sc == mx, blk_f, float(nblk)), axis=1, keepdims=True).astype(jnp.int32)
            bits = bits | jnp.where(mx > 0.5 * NEG, jnp.left_shift(1, first), 0)
            sc = jnp.where(blk == first, NEG, sc)
        bits_scr[h] = jnp.broadcast_to(bits, (ATT_TILE, LANES))
        qb_scr[h] = (qh * HEAD_DIM ** -0.5).astype(BF16)

    m_scr[...] = jnp.full(m_scr.shape, NEG, F32)
    l_scr[...] = jnp.zeros(l_scr.shape, F32)
    acc_scr[...] = jnp.zeros(acc_scr.shape, F32)

    def block_step(j, kind):
        k_j, v_j = _kv_tile(kv_ref, j)
        alphas, pvs = [], []
        for h in range(N_HEADS):
            s = jnp.dot(qb_scr[h], k_j, preferred_element_type=F32)
            if kind == "diag":
                s = jnp.where(_causal_tile(), s + bias_ref[h, 0], NEG)
            else:
                s = s + (bias_ref[h, 1] if kind == "prev" else tab_ref[N_BUCKETS - 1, h])
                picked = (jnp.right_shift(bits_scr[h], j) & 1) == 1
                s = jnp.where(_widen(picked, ATT_TILE), s, NEG)
            p, alpha = _softmax_step(s, m_scr, l_scr, h)
            alphas.append(_widen(alpha, MIX_W))
            pvs.append(lax.dot_general(p.astype(BF16), v_j, NT_DIMS, preferred_element_type=F32))
        acc_scr[...] = acc_scr[...] * _per_head(alphas, hid) + _per_head(pvs, hid)

    block_step(own, "diag")

    @pl.when(own >= 1)
    def _():
        block_step(own - 1, "prev")

    def far(j, carry):
        block_step(j, "far")
        return carry

    lax.fori_loop(0, jnp.maximum(own - 1, 0), far, 0)
    denom = _per_head([_widen(l_scr[h], MIX_W) for h in range(N_HEADS)], hid)
    o_ref[...] = (acc_scr[...] / denom).astype(BF16)


def _moba_prompt(tab, q, kvb, km, bias):
    b, s, _ = q.shape
    nblk = s // ATT_TILE
    assert nblk <= 31
    return pl.pallas_call(
        functools.partial(_moba_body, nblk=nblk),
        grid=(b, nblk),
        in_specs=[_smem_spec(),
                  pl.BlockSpec((None, ATT_TILE, MIX_W), lambda bi, i: (bi, i, 0)),
                  pl.BlockSpec((None, nblk, 2 * MIX_W, ATT_TILE), lambda bi, i: (bi, 0, 0, 0)),
                  pl.BlockSpec((None, MIX_W, nblk), lambda bi, i: (bi, 0, 0)),
                  _const_spec((N_HEADS, 2, ATT_TILE, ATT_TILE))],
        out_specs=pl.BlockSpec((None, ATT_TILE, MIX_W), lambda bi, i: (bi, i, 0)),
        out_shape=jax.ShapeDtypeStruct((b, s, MIX_W), BF16),
        scratch_shapes=[pltpu.VMEM((N_HEADS, ATT_TILE, LANES), F32), pltpu.VMEM((N_HEADS, ATT_TILE, LANES), F32),
                        pltpu.VMEM((ATT_TILE, MIX_W), F32), pltpu.VMEM((N_HEADS, ATT_TILE, LANES), jnp.int32),
                        pltpu.VMEM((N_HEADS, ATT_TILE, MIX_W), BF16)],
        compiler_params=pltpu.CompilerParams(dimension_semantics=("arbitrary", "arbitrary")),
        name="moba_prompt",
    )(tab, q, kvb, km, bias)


def _diff_lambda(lam_ref, lam_init):
    lp = lam_ref[...]
    a = jnp.sum(lp[0:1] * lp[1:2], axis=1, keepdims=True)
    b = jnp.sum(lp[2:3] * lp[3:4], axis=1, keepdims=True)
    return jnp.exp(a) - jnp.exp(b) + lam_init


def _subln(o, g, hid, lam_init):
    sq = o * o
    ms = _per_head([jnp.sum(jnp.where(hid == h, sq, 0.0), axis=-1, keepdims=True) * (1.0 / HEAD_DIM)
                    for h in range(N_HEADS)], hid)
    return o * lax.rsqrt(ms + EPS) * g * (1.0 - lam_init)


def _diff_body(tab_ref, lam_ref, q_ref, kv_ref, bias_ref, g_ref, o_ref, m_scr, l_scr, acc_scr, qb_scr, *, lam_init):
    i = pl.program_id(1)
    q = q_ref[...]
    lane = lax.broadcasted_iota(jnp.int32, (1, MIX_W), 1)
    hid = lane // HEAD_DIM
    seg = lane // DIFF_QK
    for c in range(2):
        for h in range(N_HEADS):
            qb_scr[c * N_HEADS + h] = jnp.where(seg == 2 * h + c, q, jnp.zeros_like(q))
    m_scr[...] = jnp.full(m_scr.shape, NEG, F32)
    l_scr[...] = jnp.zeros(l_scr.shape, F32)
    acc_scr[...] = jnp.zeros(acc_scr.shape, F32)

    def block_step(j, kind):
        k_j, v_j = _kv_tile(kv_ref, j)
        for c in range(2):
            alphas, pvs = [], []
            for h in range(N_HEADS):
                idx = c * N_HEADS + h
                s = jnp.dot(qb_scr[idx], k_j, preferred_element_type=F32)
                if kind == "diag":
                    s = jnp.where(_causal_tile(), s + bias_ref[h, 0], NEG)
                else:
                    s = s + (bias_ref[h, 1] if kind == "prev" else tab_ref[N_BUCKETS - 1, N_HEADS + h])
                p, alpha = _softmax_step(s, m_scr, l_scr, idx)
                alphas.append(_widen(alpha, MIX_W))
                pvs.append(lax.dot_general(p.astype(BF16), v_j, NT_DIMS, preferred_element_type=F32))
            acc_scr[c] = acc_scr[c] * _per_head(alphas, hid) + _per_head(pvs, hid)

    block_step(i, "diag")

    @pl.when(i >= 1)
    def _():
        block_step(i - 1, "prev")

    def far(j, carry):
        block_step(j, "far")
        return carry

    lax.fori_loop(0, jnp.maximum(i - 1, 0), far, 0)
    outs = [acc_scr[c] / _per_head([_widen(l_scr[c * N_HEADS + h], MIX_W) for h in range(N_HEADS)], hid)
            for c in range(2)]
    o = outs[0] - _diff_lambda(lam_ref, lam_init) * outs[1]
    o_ref[...] = _subln(o, g_ref[...], hid, lam_init).astype(BF16)


def _diff_prompt(tab, lam_p, q, kvb, bias, g_full, *, lam_init):
    b, s, _ = q.shape
    nt = s // ATT_TILE
    return pl.pallas_call(
        functools.partial(_diff_body, lam_init=lam_init),
        grid=(b, nt),
        in_specs=[_smem_spec(), _const_spec(lam_p.shape),
                  pl.BlockSpec((None, ATT_TILE, MIX_W), lambda bi, i: (bi, i, 0)),
                  pl.BlockSpec((None, nt, 2 * MIX_W, ATT_TILE), lambda bi, i: (bi, 0, 0, 0)),
                  pl.BlockSpec((N_HEADS, 2, ATT_TILE, ATT_TILE), lambda bi, i: (1, 0, 0, 0),
                               pipeline_mode=pl.Buffered(1)),
                  _const_spec((1, MIX_W))],
        out_specs=pl.BlockSpec((None, ATT_TILE, MIX_W), lambda bi, i: (bi, i, 0)),
        out_shape=jax.ShapeDtypeStruct((b, s, MIX_W), BF16),
        scratch_shapes=[pltpu.VMEM((N_SEG, ATT_TILE, LANES), F32), pltpu.VMEM((N_SEG, ATT_TILE, LANES), F32),
                        pltpu.VMEM((2, ATT_TILE, MIX_W), F32), pltpu.VMEM((N_SEG, ATT_TILE, MIX_W), BF16)],
        compiler_params=pltpu.CompilerParams(dimension_semantics=("arbitrary", "arbitrary")),
        name="diff_prompt",
    )(tab, lam_p, q, kvb, bias, g_full)


def _inproj_s_body(x_ref, g_ref, w_ref, wcs_ref, wcc_ref, lng_ref, lnb_ref, sts_ref, stc_ref,
                   qa_ref, kvm_ref, qd_ref, kvd_ref, ob_ref, oc_ref, ub_ref, uc_ref, *, sc_w, cf_w):
    h = _rms(x_ref[...], g_ref[...]).astype(BF16)

    def part(n):
        return jnp.dot(h, w_ref[:, n * MIX_W:(n + 1) * MIX_W], preferred_element_type=F32)

    def conv(st_ref, u, wc_ref, width):
        acc = None
        for k in range(width - 1):
            term = st_ref[k] * wc_ref[k:k + 1, :]
            acc = term if acc is None else acc + term
        return acc + u * wc_ref[width - 1:width, :]

    qa_ref[...] = part(0)
    kvm_ref[:, 0:MIX_W] = part(1)
    kvm_ref[:, MIX_W:2 * MIX_W] = part(2)
    qd_ref[...] = part(3)
    kvd_ref[:, 0:MIX_W] = part(4)
    kvd_ref[:, MIX_W:2 * MIX_W] = part(5)
    ub = part(8) * part(6)
    ub_ref[...] = ub
    ob_ref[...] = (part(7) * conv(sts_ref, ub, wcs_ref, sc_w)).astype(BF16)
    uc = part(9) * jax.nn.sigmoid(part(10))
    uc_ref[...] = uc
    yc = conv(stc_ref, uc, wcc_ref, cf_w)
    mu = jnp.mean(yc, axis=-1, keepdims=True)
    xc = yc - mu
    yn = xc * lax.rsqrt(jnp.mean(xc * xc, axis=-1, keepdims=True) + EPS) * lng_ref[...] + lnb_ref[...]
    oc_ref[...] = (yn * jax.nn.sigmoid(yn)).astype(BF16)


def _inproj_sample(x, g, w_mix, w_cs, w_cc, ln_g, ln_b, st_s, st_c):
    n, _ = x.shape
    sds = jax.ShapeDtypeStruct
    return pl.pallas_call(
        functools.partial(_inproj_s_body, sc_w=w_cs.shape[0], cf_w=w_cc.shape[0]),
        out_shape=[sds((n, MIX_W), F32), sds((n, 2 * MIX_W), F32), sds((n, MIX_W), F32), sds((n, 2 * MIX_W), F32),
                   sds((n, MIX_W), BF16), sds((n, MIX_W), BF16), sds((n, MIX_W), F32), sds((n, MIX_W), F32)],
        name="inproj_sample",
    )(x, g, w_mix, w_cs, w_cc, ln_g, ln_b, st_s, st_c)


def _head_rows(q, rows):
    r = lax.broadcasted_iota(jnp.int32, (rows, MIX_W), 0)
    hid = lax.broadcasted_iota(jnp.int32, (rows, MIX_W), 1) // HEAD_DIM
    return jnp.where(r == hid, jnp.broadcast_to(q, (rows, MIX_W)), 0.0)


def _smoba_sel_body(pt_ref, q_ref, *refs, pg, nblk):
    pages, sel_ref, ksum_scr = refs[:pg], refs[pg], refs[pg + 1]
    g = pl.program_id(1)
    lane = lax.broadcasted_iota(jnp.int32, (8, LANES), 1)

    @pl.when(g == 0)
    def _():
        ksum_scr[...] = jnp.zeros(ksum_scr.shape, F32)

    col_lane = lax.broadcasted_iota(jnp.int32, ksum_scr.shape, 1)
    for k in range(pg // 2):
        k_sum = jnp.sum(pages[2 * k][...] + pages[2 * k + 1][...], axis=1, keepdims=True)
        ksum_scr[...] = jnp.where(col_lane == g * (pg // 2) + k, k_sum, ksum_scr[...])

    @pl.when(g == pl.num_programs(1) - 1)
    def _():
        km = ksum_scr[...] * (1.0 / MOBA_BLOCK)
        sc = jnp.dot(_head_rows(q_ref[...], 8), km, precision=HIGHEST, preferred_element_type=F32)
        sc = jnp.where(lane < nblk, sc, NEG)
        lane_f = lane.astype(F32)
        out = jnp.zeros((8, LANES), jnp.int32)
        for r in range(MOBA_TOPK):
            mx = jnp.max(sc, axis=1, keepdims=True)
            first = jnp.min(jnp.where(sc == mx, lane_f, float(nblk - 1)), axis=1, keepdims=True).astype(jnp.int32)
            out = jnp.where(lane == r, first, out)
            sc = jnp.where(lane == first, NEG, sc)
        sel_ref[...] = out


def _smoba_select(page_table, q3, cache, layer, *, pg):
    db, n_pages = page_table.shape
    page = cache.shape[3]
    nblk = n_pages * page // MOBA_BLOCK
    assert 2 * page == MOBA_BLOCK and n_pages % pg == 0 and pg % 2 == 0 and MOBA_TOPK <= nblk <= LANES

    def page_spec(k):
        return pl.BlockSpec((None, None, MIX_W, page), lambda b, g, pt: (layer, pt[b, g * pg + k], 0, 0))

    return pl.pallas_call(
        functools.partial(_smoba_sel_body, pg=pg, nblk=nblk),
        grid_spec=pltpu.PrefetchScalarGridSpec(
            num_scalar_prefetch=1,
            grid=(db, n_pages // pg),
            in_specs=[pl.BlockSpec((None, 1, MIX_W), lambda b, g, pt: (b, 0, 0))] + [page_spec(k) for k in range(pg)],
            out_specs=pl.BlockSpec((None, 8, LANES), lambda b, g, pt: (b, 0, 0)),
            scratch_shapes=[pltpu.VMEM((MIX_W, LANES), F32)]),
        out_shape=jax.ShapeDtypeStruct((db, 8, LANES), jnp.int32),
        compiler_params=pltpu.CompilerParams(dimension_semantics=("arbitrary", "arbitrary")),
        name="moba_sample_select",
    )(page_table, q3, *([cache] * pg))


def _smoba_attn_body(pgid_ref, sel_ref, tab_ref, q_ref, kvn_ref, *refs, page, past_len):
    n_pg = N_HEADS * MOBA_TOPK * 2
    pages, o_ref = refs[:n_pg], refs[n_pg]
    b = pl.program_id(0)
    q = q_ref[...] * HEAD_DIM ** -0.5
    q8 = _head_rows(q, 8)
    q8b = q8.astype(BF16)
    k_new = kvn_ref[:, 0:MIX_W]
    v_new = kvn_ref[:, MIX_W:2 * MIX_W]
    lane = lax.broadcasted_iota(jnp.int32, (1, page), 1)
    hid = lax.broadcasted_iota(jnp.int32, (1, MIX_W), 1) // HEAD_DIM
    outs = []
    for h in range(N_HEADS):
        logits = []
        for r in range(MOBA_TOPK):
            for half in range(2):
                pg_ref = pages[(h * MOBA_TOPK + r) * 2 + half]
                s = jnp.dot(q8b, pg_ref[0:MIX_W, :].astype(BF16), preferred_element_type=F32)
                k_pos = sel_ref[b, h * MOBA_TOPK + r] * MOBA_BLOCK + half * page + lane
                logits.append(s[h:h + 1, :] + _t5_bias(past_len - k_pos, lambda bk: tab_ref[bk, h]))
        s_own = jnp.sum(q8[h:h + 1, :] * k_new, axis=1, keepdims=True) + tab_ref[0, h]
        m = s_own
        for lg in logits:
            m = jnp.maximum(m, jnp.max(lg, axis=1, keepdims=True))
        p_own = jnp.exp(s_own - m)
        denom = p_own
        acc = p_own * v_new
        for n, lg in enumerate(logits):
            p = jnp.exp(lg - m)
            denom = denom + jnp.sum(p, axis=1, keepdims=True)
            p8 = jnp.broadcast_to(p, (8, page)).astype(BF16)
            pv = lax.dot_general(p8, pages[h * MOBA_TOPK * 2 + n][MIX_W:2 * MIX_W, :].astype(BF16), NT_DIMS,
                                 preferred_element_type=F32)
            acc = acc + pv[0:1, :]
        outs.append(acc / denom)
    o_ref[...] = _per_head(outs, hid).astype(BF16)


def _smoba_attend(page_table, sel, tab, q3, kvn3, cache, layer):
    db, n_pages = page_table.shape
    page = cache.shape[3]
    n_pg = N_HEADS * MOBA_TOPK * 2
    slots = (2 * sel[:, :, None] + jnp.arange(2, dtype=jnp.int32)).reshape(db, n_pg)
    page_ids = jnp.take_along_axis(page_table, slots, axis=1)

    def page_spec(n):
        return pl.BlockSpec((None, None, 2 * MIX_W, page), lambda b, pg, sl: (layer, pg[b, n], 0, 0))

    row = lambda w: pl.BlockSpec((None, 1, w), lambda b, pg, sl: (b, 0, 0))
    return pl.pallas_call(
        functools.partial(_smoba_attn_body, page=page, past_len=n_pages * page),
        grid_spec=pltpu.PrefetchScalarGridSpec(
            num_scalar_prefetch=2,
            grid=(db,),
            in_specs=[_smem_spec(), row(MIX_W), row(2 * MIX_W)] + [page_spec(n) for n in range(n_pg)],
            out_specs=row(MIX_W)),
        out_shape=jax.ShapeDtypeStruct((db, 1, MIX_W), BF16),
        compiler_params=pltpu.CompilerParams(dimension_semantics=("arbitrary",)),
        name="moba_sample_attend",
    )(page_ids, sel, tab, q3, kvn3, *([cache] * n_pg))


def _sdiff_body(pt_ref, tab_ref, lam_ref, q_ref, kvn_ref, g_ref, *refs, pg, page, past_len, lam_init):
    pages, o_ref, m_scr, l_scr, acc_scr = refs[:pg], refs[pg], refs[pg + 1], refs[pg + 2], refs[pg + 3]
    g = pl.program_id(1)
    row = lax.broadcasted_iota(jnp.int32, (N_SEG, MIX_W), 0)
    lane = lax.broadcasted_iota(jnp.int32, (N_SEG, MIX_W), 1)
    q8 = jnp.where(lane // DIFF_QK == 2 * (row % N_HEADS) + row // N_HEADS,
                   jnp.broadcast_to(q_ref[...] * DIFF_QK ** -0.5, (N_SEG, MIX_W)), 0.0)
    q8b = q8.astype(BF16)
    rh = lax.broadcasted_iota(jnp.int32, (N_SEG, 1), 0) % N_HEADS

    def tab_rows(bk):
        return _per_head([tab_ref[bk, N_HEADS + h] for h in range(N_HEADS)], rh)

    @pl.when(g == 0)
    def _():
        m_scr[...] = jnp.full(m_scr.shape, NEG, F32)
        l_scr[...] = jnp.zeros(l_scr.shape, F32)
        acc_scr[...] = jnp.zeros(acc_scr.shape, F32)

    pos = lax.broadcasted_iota(jnp.int32, (1, page), 1)
    logits = []
    for k in range(pg):
        s = jnp.dot(q8b, pages[k][0:MIX_W, :].astype(BF16), preferred_element_type=F32)
        logits.append(s + _t5_bias(past_len - ((g * pg + k) * page + pos), tab_rows))
    s = jnp.concatenate(logits, axis=1)
    m_prev = m_scr[...]
    m_new = jnp.maximum(m_prev, jnp.max(s, axis=1, keepdims=True))
    p = jnp.exp(s - _widen(m_new, pg * page))
    alpha = jnp.exp(m_prev - m_new)
    l_scr[...] = alpha * l_scr[...] + jnp.sum(p, axis=1, keepdims=True)
    m_scr[...] = m_new
    pv = None
    for k in range(pg):
        term = lax.dot_general(p[:, k * page:(k + 1) * page].astype(BF16), pages[k][MIX_W:2 * MIX_W, :].astype(BF16),
                               NT_DIMS, preferred_element_type=F32)
        pv = term if pv is None else pv + term
    acc_scr[...] = acc_scr[...] * _widen(alpha, MIX_W) + pv

    @pl.when(g == pl.num_programs(1) - 1)
    def _():
        s_own = jnp.sum(q8 * kvn_ref[:, 0:MIX_W], axis=1, keepdims=True) + tab_rows(0)
        m_last = m_scr[:, 0:1]
        m_fin = jnp.maximum(m_last, s_own)
        a = jnp.exp(m_last - m_fin)
        p_own = jnp.exp(s_own - m_fin)
        denom = a * l_scr[:, 0:1] + p_own
        o8 = (acc_scr[...] * a + p_own * kvn_ref[:, MIX_W:2 * MIX_W]) / denom
        lam = _diff_lambda(lam_ref, lam_init)
        mine = lane // HEAD_DIM == row % N_HEADS
        o = jnp.sum(jnp.where(mine, jnp.where(row < N_HEADS, o8, -lam * o8), 0.0), axis=0, keepdims=True)
        hid = lax.broadcasted_iota(jnp.int32, (1, MIX_W), 1) // HEAD_DIM
        o_ref[...] = _subln(o, g_ref[...], hid, lam_init).astype(BF16)


def _sdiff_attend(page_table, tab, lam_p, q3, kvn3, g_full, cache, layer, *, pg, lam_init):
    db, n_pages = page_table.shape
    page = cache.shape[3]
    assert n_pages % pg == 0 and page == LANES

    def page_spec(k):
        return pl.BlockSpec((None, None, 2 * MIX_W, page), lambda b, g, pt: (layer, pt[b, g * pg + k], 0, 0))

    row = lambda w: pl.BlockSpec((None, 1, w), lambda b, g, pt: (b, 0, 0))
    return pl.pallas_call(
        functools.partial(_sdiff_body, pg=pg, page=page, past_len=n_pages * page, lam_init=lam_init),
        grid_spec=pltpu.PrefetchScalarGridSpec(
            num_scalar_prefetch=1,
            grid=(db, n_pages // pg),
            in_specs=[_smem_spec(), pl.BlockSpec(lam_p.shape, lambda b, g, pt: (0, 0)),
                      row(MIX_W), row(2 * MIX_W), pl.BlockSpec((1, MIX_W), lambda b, g, pt: (0, 0))]
                     + [page_spec(k) for k in range(pg)],
            out_specs=row(MIX_W),
            scratch_shapes=[pltpu.VMEM((N_SEG, LANES), F32), pltpu.VMEM((N_SEG, LANES), F32),
                            pltpu.VMEM((N_SEG, MIX_W), F32)]),
        out_shape=jax.ShapeDtypeStruct((db, 1, MIX_W), BF16),
        compiler_params=pltpu.CompilerParams(dimension_semantics=("arbitrary", "arbitrary")),
        name="diff_sample_attend",
    )(page_table, tab, lam_p, q3, kvn3, g_full, *([cache] * pg))


def _row_tile(n):
    return 512 if n % 512 == 0 else n


def kernel(x_prompt, x_sample, cache_kv_moba, cache_kv_diff, state_conv_short, state_conv_conformer, page_table, rel_bias_table, norm_g, w_ffn1_up, w_ffn1_down, w_ffn2_up, w_ffn2_down, w_in, diff_lambda, diff_subln_g, w_conv_short, w_conv_cf, cf_ln_g, cf_ln_b, w_branch, w_out):
    nb, s, d = x_prompt.shape
    db, q_len, _ = x_sample.shape
    depth, n_pool, page = cache_kv_moba.shape[:3]
    assert q_len == 1 and w_in.shape[-1] == N_SPLIT * MIX_W + N_BRANCH * d
    assert rel_bias_table.shape == (N_BUCKETS, 2 * N_HEADS) and (page_table.shape[1] * page) % MOBA_BLOCK == 0
    n_mix = N_SPLIT * MIX_W
    tm_p, tm_s = _row_tile(nb * s), db
    tm_in = 512 if s % 512 == 0 else MOBA_BLOCK

    bias = _bias_tiles(rel_bias_table)
    cache_m = jnp.transpose(cache_kv_moba, (0, 1, 3, 4, 5, 2)).reshape(depth, n_pool, 2 * MIX_W, page)
    cache_d = jnp.transpose(cache_kv_diff, (0, 1, 3, 4, 5, 2)).reshape(depth, n_pool, 2 * MIX_W, page)
    kv_out = lambda a: jnp.transpose(a.reshape(nb, 2, N_HEADS, HEAD_DIM, s), (0, 4, 1, 2, 3))
    yp = x_prompt.reshape(nb * s, d)
    ys = x_sample.reshape(db, d)
    outs = [[] for _ in range(8)]
    for l in range(depth):
        g = norm_g[l]
        lam_init = 0.8 - 0.6 * math.exp(-0.3 * l)
        w_mix = w_in[l][:, :n_mix].astype(BF16)
        w_gate = w_in[l][:, n_mix:].astype(BF16)
        w_kv_t = jnp.concatenate([w_in[l][:, MIX_W:3 * MIX_W], w_in[l][:, 4 * MIX_W:6 * MIX_W]], axis=1).T.astype(BF16)
        w_br = w_branch[l].astype(BF16)
        w_o = w_out[l].astype(BF16)
        w1u, w1d = w_ffn1_up[l].astype(BF16), w_ffn1_down[l].astype(BF16)
        w2u, w2d = w_ffn2_up[l].astype(BF16), w_ffn2_down[l].astype(BF16)
        g_sub = jnp.tile(diff_subln_g[l], N_HEADS)[None, :]
        ln_g, ln_b = cf_ln_g[l][None, :], cf_ln_b[l][None, :]

        yp = _ffn(yp, g[0:2], w1u, w1d, tm=tm_p)
        ys = _ffn(ys, g[0:2], w1u, w1d, tm=tm_s)

        qa, kvm, kvmb, km, qd, kvd, kvdb, ob, oc, cs, cc = _inproj(
            yp.reshape(nb, s, d), g[2:3], w_mix, w_kv_t, w_conv_short[l], w_conv_cf[l], ln_g, ln_b, tm=tm_in)
        oa = _moba_prompt(rel_bias_table, qa, kvmb, km, bias)
        od = _diff_prompt(rel_bias_table, diff_lambda[l], qd, kvdb, bias, g_sub, lam_init=lam_init)
        flat = lambda a: a.reshape(nb * s, MIX_W)
        yp = _merge(yp, flat(oa), flat(ob), flat(oc), flat(od), g[2:4], w_gate, w_br, w_o, tm=tm_p)
        outs[0].append(kv_out(kvm))
        outs[1].append(kv_out(kvd))
        outs[2].append(cs)
        outs[3].append(cc)

        st_s = jnp.transpose(state_conv_short[l], (1, 0, 2))
        st_c = jnp.transpose(state_conv_conformer[l], (1, 0, 2))
        qa_s, kvm_s, qd_s, kvd_s, ob_s, oc_s, ub_s, uc_s = _inproj_sample(
            ys, g[2:3], w_mix, w_conv_short[l], w_conv_cf[l], ln_g, ln_b, st_s, st_c)
        sel = _smoba_select(page_table, qa_s[:, None, :], cache_m, l, pg=16)
        sel = sel[:, :N_HEADS, :MOBA_TOPK].reshape(db, N_HEADS * MOBA_TOPK)
        oa_s = _smoba_attend(page_table, sel, rel_bias_table, qa_s[:, None, :], kvm_s[:, None, :], cache_m, l)
        od_s = _sdiff_attend(page_table, rel_bias_table, diff_lambda[l], qd_s[:, None, :], kvd_s[:, None, :],
                             g_sub, cache_d, l, pg=8, lam_init=lam_init)
        ys = _merge(ys, oa_s.reshape(db, MIX_W), ob_s, oc_s, od_s.reshape(db, MIX_W), g[2:4], w_gate, w_br, w_o,
                    tm=tm_s)
        outs[4].append(kvm_s.reshape(db, 1, 2, N_HEADS, HEAD_DIM))
        outs[5].append(kvd_s.reshape(db, 1, 2, N_HEADS, HEAD_DIM))
        outs[6].append(jnp.transpose(jnp.concatenate([st_s[1:], ub_s[None]], axis=0), (1, 0, 2)))
        outs[7].append(jnp.transpose(jnp.concatenate([st_c[1:], uc_s[None]], axis=0), (1, 0, 2)))

        yp = _ffn(yp, g[4:6], w2u, w2d, tm=tm_p)
        ys = _ffn(ys, g[4:6], w2u, w2d, tm=tm_s)

    return (yp.reshape(nb, s, d), ys.reshape(db, 1, d)) + tuple(jnp.stack(o) for o in outs)
```
